```python
import math
import jax, jax.numpy as jnp
from jax import lax
import numpy as np

D_MODEL = 1024
BATCH = 16
SEQ = 2048
DEPTH = 4

HEAD_DIM = 64
SB_HEADS = 8
SB_WIDTH = SB_HEADS * HEAD_DIM
NSA_HEADS = 8
NSA_KV_HEADS = 2
NSA_GROUP = NSA_HEADS // NSA_KV_HEADS
NSA_WIDTH = NSA_HEADS * HEAD_DIM
NSA_KV_WIDTH = NSA_KV_HEADS * HEAD_DIM
NSA_BRANCHES = 3
CMP_LEN = 32
CMP_STRIDE = 16
CMP_HIDDEN = 256
SEL_BLOCK = 64
SEL_TOP_N = 8
WINDOW = 512
Q_BLOCK = 128
ROPE_THETA = 10000.0
NORM_EPS = 1e-6
FORCED_BONUS = 1e4
NEG_INF = -1e30

IN_SPLITS = (SB_WIDTH, SB_WIDTH, SB_WIDTH, SB_WIDTH,
             NSA_WIDTH,
             NSA_KV_WIDTH, NSA_KV_WIDTH,
             NSA_KV_WIDTH, NSA_KV_WIDTH,
             NSA_KV_WIDTH, NSA_KV_WIDTH,
             NSA_BRANCHES * NSA_HEADS,
             NSA_WIDTH,
             D_MODEL, D_MODEL)
N_IN = 4 * SB_WIDTH + NSA_WIDTH + 6 * NSA_KV_WIDTH + NSA_BRANCHES * NSA_HEADS + NSA_WIDTH + 2 * D_MODEL

kernel_name = 'hybrid_stickbreak_nsa_block'


def rms_norm(x, g):
    xf = x.astype(jnp.float32)
    y = xf * lax.rsqrt(jnp.mean(xf * xf, axis=-1, keepdims=True) + NORM_EPS)
    return (y * g.astype(jnp.float32)).astype(x.dtype)


def rope(x, pos):
    half = HEAD_DIM // 2
    inv_freq = ROPE_THETA ** (-jnp.arange(half, dtype=jnp.float32) / half)
    ang = pos.astype(jnp.float32)[:, None, :, None] * inv_freq
    cos, sin = jnp.cos(ang), jnp.sin(ang)
    xf = x.astype(jnp.float32)
    x1, x2 = xf[..., :half], xf[..., half:]
    return jnp.concatenate([x1 * cos - x2 * sin, x2 * cos + x1 * sin], axis=-1).astype(x.dtype)


def split_heads(t, n):
    b, l, _ = t.shape
    return t.reshape(b, l, n, HEAD_DIM).transpose(0, 2, 1, 3)


def merge_heads(t):
    b, n, l, d = t.shape
    return t.transpose(0, 2, 1, 3).reshape(b, l, n * d)


def masked_softmax(s, mask):
    return jax.nn.softmax(jnp.where(mask, s.astype(jnp.float32), NEG_INF), axis=-1)


def stick_breaking_attention(q, k, v):
    s_len = q.shape[2]
    scale = HEAD_DIM ** -0.5
    outs = []
    for i in range(s_len // Q_BLOCK):
        end = (i + 1) * Q_BLOCK
        qb = q[:, :, i * Q_BLOCK:end]
        kb, vb = k[:, :, :end], v[:, :, :end]
        z = jnp.einsum('bhqd,bhkd->bhqk', qb, kb).astype(jnp.float32) * scale
        t = i * Q_BLOCK + jnp.arange(Q_BLOCK)
        s = jnp.arange(end)
        mask = s[None, :] < t[:, None]
        log_1m = jnp.where(mask, jax.nn.log_sigmoid(-z), 0.0)
        between = lax.cumsum(log_1m, axis=3, reverse=True) - log_1m
        w = jnp.where(mask, jnp.exp(jax.nn.log_sigmoid(z) + between), 0.0)
        outs.append(jnp.einsum('bhqk,bhkd->bhqd', w.astype(vb.dtype), vb))
    return jnp.concatenate(outs, axis=2)


def compress_blocks(x, idx, pe, w1, b1, w2):
    blocks = x[:, :, idx] + pe
    flat = blocks.reshape(blocks.shape[:3] + (CMP_LEN * HEAD_DIM,))
    return jax.nn.silu(flat @ w1 + b1) @ w2


def nsa_attention(q, k_cmp, v_cmp, k_slc, v_slc, k_win, v_win, branch_gate, pos,
                  k_cmp_norm_g, cmp_pe, cmp_w1, cmp_b1, cmp_w2):
    b, _, s_len, _ = q.shape
    g_n, r_n = NSA_KV_HEADS, NSA_GROUP
    scale = HEAD_DIM ** -0.5
    t = jnp.arange(s_len)
    qg = q.reshape(b, g_n, r_n, s_len, HEAD_DIM)

    n_cmp = (s_len - CMP_LEN) // CMP_STRIDE + 1
    cmp_start = jnp.arange(n_cmp) * CMP_STRIDE
    idx = cmp_start[:, None] + jnp.arange(CMP_LEN)[None, :]
    kc = compress_blocks(k_cmp, idx, cmp_pe[0], cmp_w1[0], cmp_b1[0], cmp_w2[0])
    vc = compress_blocks(v_cmp, idx, cmp_pe[1], cmp_w1[1], cmp_b1[1], cmp_w2[1])
    kc = rope(rms_norm(kc, k_cmp_norm_g), jnp.mean(pos[:, idx].astype(jnp.float32), axis=-1))
    s_cmp = jnp.einsum('bgrqd,bgcd->bgrqc', qg, kc) * scale
    mask_c = (cmp_start + CMP_LEN - 1)[None, :] <= t[:, None]
    p_cmp = masked_softmax(s_cmp, mask_c) * jnp.any(mask_c, axis=-1)[:, None]
    o_cmp = jnp.einsum('bgrqc,bgcd->bgrqd', p_cmp.astype(vc.dtype), vc)

    n_sel = s_len // SEL_BLOCK
    top_n = min(SEL_TOP_N, n_sel)
    sel_start = jnp.arange(n_sel) * SEL_BLOCK
    overlap = ((cmp_start[:, None] < sel_start[None, :] + SEL_BLOCK)
               & (cmp_start[:, None] + CMP_LEN > sel_start[None, :])).astype(jnp.float32)
    imp = jnp.einsum('bgrqc,cj->bgqj', p_cmp, overlap)
    cur = t // SEL_BLOCK
    j = jnp.arange(n_sel)
    forced = (j[None, :] == 0) | (j[None, :] == cur[:, None]) | (j[None, :] == cur[:, None] - 1)
    imp = jnp.where(sel_start[None, :] <= t[:, None],
                    imp + jnp.where(forced, FORCED_BONUS, 0.0), NEG_INF)
    _, sel_idx = lax.top_k(imp, top_n)

    nb = s_len // Q_BLOCK
    q_blocks = qg.reshape(b, g_n, r_n, nb, Q_BLOCK, HEAD_DIM).transpose(3, 0, 1, 2, 4, 5)
    idx_blocks = sel_idx.reshape(b, g_n, nb, Q_BLOCK, top_n).transpose(2, 0, 1, 3, 4)
    pad = ((0, 0), (0, 0), (WINDOW, 0), (0, 0))
    kw_pad, vw_pad = jnp.pad(k_win, pad), jnp.pad(v_win, pad)
    gather = jax.vmap(jax.vmap(lambda arr, ii: arr[ii]))
    n_keys = top_n * SEL_BLOCK

    def block_step(args):
        qb, ib, i = args
        tq = i * Q_BLOCK + jnp.arange(Q_BLOCK)
        tok = (ib[..., None] * SEL_BLOCK + jnp.arange(SEL_BLOCK)).reshape(b, g_n, Q_BLOCK * n_keys)
        ks = gather(k_slc, tok).reshape(b, g_n, Q_BLOCK, n_keys, HEAD_DIM)
        vs = gather(v_slc, tok).reshape(b, g_n, Q_BLOCK, n_keys, HEAD_DIM)
        tok = tok.reshape(b, g_n, Q_BLOCK, n_keys)
        ss = jnp.einsum('bgrqd,bgqkd->bgrqk', qb, ks) * scale
        ps = masked_softmax(ss, (tok <= tq[:, None])[:, :, None])
        o_s = jnp.einsum('bgrqk,bgqkd->bgrqd', ps.astype(vs.dtype), vs)
        kw = lax.dynamic_slice_in_dim(kw_pad, i * Q_BLOCK, WINDOW + Q_BLOCK, axis=2)
        vw = lax.dynamic_slice_in_dim(vw_pad, i * Q_BLOCK, WINDOW + Q_BLOCK, axis=2)
        p = i * Q_BLOCK - WINDOW + jnp.arange(WINDOW + Q_BLOCK)
        mw = (p[None, :] <= tq[:, None]) & (p[None, :] > tq[:, None] - WINDOW) & (p[None, :] >= 0)
        sw = jnp.einsum('bgrqd,bgkd->bgrqk', qb, kw) * scale
        pw = masked_softmax(sw, mw)
        o_w = jnp.einsum('bgrqk,bgkd->bgrqd', pw.astype(vw.dtype), vw)
        return o_s, o_w

    o_slc, o_win = lax.map(block_step, (q_blocks, idx_blocks, jnp.arange(nb)))
    o_slc = o_slc.transpose(1, 2, 3, 0, 4, 5).reshape(b, g_n, r_n, s_len, HEAD_DIM)
    o_win = o_win.transpose(1, 2, 3, 0, 4, 5).reshape(b, g_n, r_n, s_len, HEAD_DIM)

    gates = jax.nn.sigmoid(branch_gate.astype(jnp.float32)).reshape(b, s_len, NSA_BRANCHES, g_n, r_n)
    gates = gates.transpose(2, 0, 3, 4, 1)[..., None]
    o = gates[0] * o_cmp + gates[1] * o_slc + gates[2] * o_win
    return o.transpose(0, 3, 1, 2, 4).reshape(b, s_len, NSA_WIDTH)


def hybrid_layer(x, pos, norm_g, w_in, q_norm_g, k_norm_g, cmp_pe, cmp_w1, cmp_b1, cmp_w2,
                 w_up_a, w_up_b, w_out):
    h = rms_norm(x, norm_g)
    proj = h @ w_in
    (sb_q, sb_k, sb_v, sb_z, n_q, n_kc, n_vc, n_ks, n_vs, n_kw, n_vw,
     n_gate, n_z, gate_a, gate_b) = jnp.split(proj, list(np.cumsum(IN_SPLITS)[:-1]), axis=-1)

    o_a = stick_breaking_attention(split_heads(sb_q, SB_HEADS), split_heads(sb_k, SB_HEADS),
                                   split_heads(sb_v, SB_HEADS))
    y_a = merge_heads(o_a) * jax.nn.silu(sb_z)

    q = rope(rms_norm(split_heads(n_q, NSA_HEADS), q_norm_g), pos)
    k_slc = rope(rms_norm(split_heads(n_ks, NSA_KV_HEADS), k_norm_g[1]), pos)
    k_win = rope(rms_norm(split_heads(n_kw, NSA_KV_HEADS), k_norm_g[2]), pos)
    o_b = nsa_attention(q, split_heads(n_kc, NSA_KV_HEADS), split_heads(n_vc, NSA_KV_HEADS),
                        k_slc, split_heads(n_vs, NSA_KV_HEADS),
                        k_win, split_heads(n_vw, NSA_KV_HEADS),
                        n_gate, pos, k_norm_g[0], cmp_pe, cmp_w1, cmp_b1, cmp_w2)
    y_b = o_b * jax.nn.silu(n_z)

    merged = jax.nn.sigmoid(gate_a) * (y_a @ w_up_a) + jax.nn.sigmoid(gate_b) * (y_b @ w_up_b)
    return (x + merged @ w_out).astype(x.dtype)


def setup_inputs(seed: int = 0) -> dict:
    key = jax.random.key(seed)
    ks = jax.random.split(key, 14)
    f32 = jnp.float32
    x = jax.random.normal(ks[0], (BATCH, SEQ, D_MODEL), f32)
    offsets = jax.random.randint(ks[1], (BATCH, 1), 0, SEQ, dtype=jnp.int32)
    positions = (jnp.arange(SEQ, dtype=jnp.int32)[None, :] + offsets).astype(jnp.int32)
    norm_g = 1.0 + 0.02 * jax.random.normal(ks[2], (DEPTH, D_MODEL), f32)
    w_in = jax.random.normal(ks[3], (DEPTH, D_MODEL, N_IN), f32) * D_MODEL ** -0.5
    q_norm_g = 1.0 + 0.02 * jax.random.normal(ks[4], (DEPTH, HEAD_DIM), f32)
    k_norm_g = 1.0 + 0.02 * jax.random.normal(ks[5], (DEPTH, NSA_BRANCHES, HEAD_DIM), f32)
    cmp_pe = 0.1 * jax.random.normal(ks[6], (DEPTH, 2, CMP_LEN, HEAD_DIM), f32)
    cmp_w1 = jax.random.normal(ks[7], (DEPTH, 2, CMP_LEN * HEAD_DIM, CMP_HIDDEN), f32) * (CMP_LEN * HEAD_DIM) ** -0.5
    cmp_b1 = 0.01 * jax.random.normal(ks[8], (DEPTH, 2, CMP_HIDDEN), f32)
    cmp_w2 = jax.random.normal(ks[9], (DEPTH, 2, CMP_HIDDEN, HEAD_DIM), f32) * CMP_HIDDEN ** -0.5
    w_up_a = jax.random.normal(ks[10], (DEPTH, SB_WIDTH, D_MODEL), f32) * SB_WIDTH ** -0.5
    w_up_b = jax.random.normal(ks[11], (DEPTH, NSA_WIDTH, D_MODEL), f32) * NSA_WIDTH ** -0.5
    w_out = jax.random.normal(ks[12], (DEPTH, D_MODEL, D_MODEL), f32) * D_MODEL ** -0.5
    return {'x': x, 'positions': positions, 'norm_g': norm_g, 'w_in': w_in,
            'q_norm_g': q_norm_g, 'k_norm_g': k_norm_g, 'cmp_pe': cmp_pe,
            'cmp_w1': cmp_w1, 'cmp_b1': cmp_b1, 'cmp_w2': cmp_w2,
            'w_up_a': w_up_a, 'w_up_b': w_up_b, 'w_out': w_out}


def reference(x, positions, norm_g, w_in, q_norm_g, k_norm_g, cmp_pe, cmp_w1, cmp_b1, cmp_w2,
              w_up_a, w_up_b, w_out):
    for layer in range(DEPTH):
        x = hybrid_layer(x, positions, norm_g[layer], w_in[layer], q_norm_g[layer], k_norm_g[layer],
                         cmp_pe[layer], cmp_w1[layer], cmp_b1[layer], cmp_w2[layer],
                         w_up_a[layer], w_up_b[layer], w_out[layer])
    return x
```

```python
import functools

import jax
import jax.numpy as jnp
from jax import lax
from jax.experimental import pallas as pl
from jax.experimental.pallas import tpu as pltpu

D_MODEL = 1024
DEPTH = 4
HEAD_DIM = 64
SB_HEADS = 8
SB_WIDTH = SB_HEADS * HEAD_DIM
NSA_HEADS = 8
NSA_KV_HEADS = 2
NSA_GROUP = NSA_HEADS // NSA_KV_HEADS
NSA_WIDTH = NSA_HEADS * HEAD_DIM
NSA_KV_WIDTH = NSA_KV_HEADS * HEAD_DIM
NSA_BRANCHES = 3
CMP_LEN = 32
CMP_STRIDE = 16
CMP_HIDDEN = 256
SEL_BLOCK = 64
SEL_TOP_N = 8
WINDOW = 512
ROPE_THETA = 10000.0
NORM_EPS = 1e-6
FORCED_BONUS = 1e4
NEG_INF = -1e30

LANES = 128
KEY_BLOCK = 128
SLC_CHUNK = 512
VMEM_LIMIT = 56 * 1024 * 1024

COL_SB_Q, COL_SB_K, COL_SB_V, COL_SB_Z = 0, 512, 1024, 1536
COL_N_Q = 2048
COL_N_KC, COL_N_VC, COL_N_KS, COL_N_VS, COL_N_KW, COL_N_VW = 2560, 2688, 2816, 2944, 3072, 3200
COL_N_GATE = 3328
COL_N_Z = 3584
COL_GATE_A, COL_GATE_B = 4096, 5120
N_PAD = 6144
N_IN_ORIG_GATE = 3328
N_GATE = NSA_BRANCHES * NSA_HEADS

F32 = jnp.float32
BF16 = jnp.bfloat16


def _cparams(sem):
    return pltpu.CompilerParams(dimension_semantics=sem, vmem_limit_bytes=VMEM_LIMIT)


def _dot(a, b):
    return jnp.dot(a, b, preferred_element_type=F32)


def _dot_nt(a, b):
    return lax.dot_general(a, b, (((1,), (1,)), ((), ())), preferred_element_type=F32)


def _split_dot(x, w):
    hi = x.astype(BF16)
    lo = (x - hi.astype(F32)).astype(BF16)
    return _dot(hi, w) + _dot(lo, w)


def _lane_iota(shape):
    return lax.broadcasted_iota(jnp.int32, shape, len(shape) - 1)


def _rope_table_kernel(pos_ref, freq_ref, cos_ref, sin_ref):
    ang = pos_ref[0] * freq_ref[...]
    first = (_lane_iota(ang.shape) % HEAD_DIM) < (HEAD_DIM // 2)
    cos_ref[0] = jnp.cos(ang)
    sin_ref[0] = jnp.where(first, -jnp.sin(ang), jnp.sin(ang))


def _rope_tables(pos_f, freq, ts):
    b, s, _ = pos_f.shape
    spec = pl.BlockSpec((1, ts, LANES), lambda i, j: (i, j, 0))
    return pl.pallas_call(
        _rope_table_kernel,
        out_shape=(jax.ShapeDtypeStruct((b, s, LANES), F32),) * 2,
        grid=(b, s // ts),
        in_specs=[pl.BlockSpec((1, ts, 1), lambda i, j: (i, j, 0)),
                  pl.BlockSpec((1, LANES), lambda i, j: (0, 0))],
        out_specs=(spec, spec),
        compiler_params=_cparams(("parallel", "parallel")),
        name="rope_tables",
    )(pos_f, freq)


def _rope(x, cos, sin_signed):
    first = (_lane_iota(x.shape) % HEAD_DIM) < (HEAD_DIM // 2)
    partner = jnp.where(first, pltpu.roll(x, LANES - HEAD_DIM // 2, 1), pltpu.roll(x, HEAD_DIM // 2, 1))
    return x * cos + partner * sin_signed


def _head_rms(x, g):
    low = _lane_iota(x.shape) < HEAD_DIM
    x2 = x * x
    s0 = jnp.sum(jnp.where(low, x2, 0.0), axis=-1, keepdims=True)
    s1 = jnp.sum(jnp.where(low, 0.0, x2), axis=-1, keepdims=True)
    ms = jnp.where(low, s0, s1) * (1.0 / HEAD_DIM)
    return x * lax.rsqrt(ms + NORM_EPS) * g


def _proj_kernel(x_ref, g_ref, w_ref, o_ref, *, n_chunk):
    x = x_ref[...]
    ms = jnp.mean(x * x, axis=-1, keepdims=True)
    h = (x * lax.rsqrt(ms + NORM_EPS) * g_ref[...]).astype(BF16)
    for c in range(N_PAD // n_chunk):
        o_ref[:, c * n_chunk:(c + 1) * n_chunk] = _dot(h, w_ref[:, c * n_chunk:(c + 1) * n_chunk])


def _proj(x2d, g, w, tm):
    m = x2d.shape[0]
    return pl.pallas_call(
        functools.partial(_proj_kernel, n_chunk=1024),
        out_shape=jax.ShapeDtypeStruct((m, N_PAD), F32),
        grid=(m // tm,),
        in_specs=[pl.BlockSpec((tm, D_MODEL), lambda i: (i, 0)),
                  pl.BlockSpec((1, D_MODEL), lambda i: (0, 0)),
                  pl.BlockSpec((D_MODEL, N_PAD), lambda i: (0, 0), pipeline_mode=pl.Buffered(1))],
        out_specs=pl.BlockSpec((tm, N_PAD), lambda i: (i, 0)),
        compiler_params=_cparams(("parallel",)),
        name="norm_in_proj",
    )(x2d, g, w)


def _sb_kernel(q_ref, k_ref, v_ref, u_ref, o_ref, ks_ref, vs_ref, *, tq):
    i = pl.program_id(2)
    nkb = ks_ref.shape[0]
    dblk = tq // KEY_BLOCK

    @pl.when(i == 0)
    def _():
        low = _lane_iota((1, LANES)) < HEAD_DIM
        k = k_ref[0]
        v = v_ref[0]
        ks_ref[:, 0:KEY_BLOCK, :] = jnp.where(low, k, 0.0).astype(BF16).reshape(nkb, KEY_BLOCK, LANES)
        ks_ref[:, KEY_BLOCK:, :] = jnp.where(low, 0.0, k).astype(BF16).reshape(nkb, KEY_BLOCK, LANES)
        vs_ref[:, 0:KEY_BLOCK, :] = jnp.where(low, v, 0.0).astype(BF16).reshape(nkb, KEY_BLOCK, LANES)
        vs_ref[:, KEY_BLOCK:, :] = jnp.where(low, 0.0, v).astype(BF16).reshape(nkb, KEY_BLOCK, LANES)

    q = (q_ref[0] * (HEAD_DIM ** -0.5)).astype(BF16)
    u = u_ref[...]

    def step(j, carry, acc, masked):
        z = _dot_nt(q, ks_ref[j])
        l1m = -(jnp.maximum(z, 0.0) + jnp.log(1.0 + jnp.exp(-jnp.abs(z))))
        if masked:
            t = i * tq + lax.broadcasted_iota(jnp.int32, z.shape, 0)
            s = j * KEY_BLOCK + (_lane_iota(z.shape) % KEY_BLOCK)
            valid = s < t
            l1m = jnp.where(valid, l1m, 0.0)
        suf = _split_dot(l1m, u)
        w = jnp.exp(z + suf[:, :2 * KEY_BLOCK] + carry)
        if masked:
            w = jnp.where(valid, w, 0.0)
        acc = acc + _dot(w.astype(BF16), vs_ref[j])
        return carry + suf[:, 2 * KEY_BLOCK:], acc

    carry = jnp.zeros((tq, 2 * KEY_BLOCK), F32)
    acc = jnp.zeros((tq, LANES), F32)
    last = (i + 1) * dblk - 1
    for d in range(dblk):
        carry, acc = step(last - d, carry, acc, True)

    def body(n, c):
        return step(i * dblk - 1 - n, c[0], c[1], False)

    carry, acc = lax.fori_loop(0, i * dblk, body, (carry, acc))
    o_ref[0] = acc


def _sb_attention(proj3, u, tq):
    b, s, _ = proj3.shape
    npair = SB_WIDTH // LANES
    nkb = s // KEY_BLOCK
    cq, ck, cv = COL_SB_Q // LANES, COL_SB_K // LANES, COL_SB_V // LANES
    return pl.pallas_call(
        functools.partial(_sb_kernel, tq=tq),
        out_shape=jax.ShapeDtypeStruct((b, s, SB_WIDTH), F32),
        grid=(b, npair, s // tq),
        in_specs=[pl.BlockSpec((1, tq, LANES), lambda bi, p, i: (bi, i, cq + p)),
                  pl.BlockSpec((1, s, LANES), lambda bi, p, i: (bi, 0, ck + p)),
                  pl.BlockSpec((1, s, LANES), lambda bi, p, i: (bi, 0, cv + p)),
                  pl.BlockSpec((2 * KEY_BLOCK, 4 * KEY_BLOCK), lambda bi, p, i: (0, 0))],
        out_specs=pl.BlockSpec((1, tq, LANES), lambda bi, p, i: (bi, i, p)),
        scratch_shapes=[pltpu.VMEM((nkb, 2 * KEY_BLOCK, LANES), BF16),
                        pltpu.VMEM((nkb, 2 * KEY_BLOCK, LANES), BF16)],
        compiler_params=_cparams(("parallel", "parallel", "arbitrary")),
        name="stick_breaking_attention",
    )(proj3, proj3, proj3, u)


def _nsa_prep_kernel(q_ref, ks_ref, vs_ref, kw_ref, vw_ref, cos_ref, sin_ref, gq_ref, gs_ref, gw_ref,
                     qo_ref, kso_ref, vso_ref, kwo_ref, vwo_ref):
    cos = cos_ref[0]
    sin = sin_ref[0]
    scale = HEAD_DIM ** -0.5
    for p in range(NSA_WIDTH // LANES):
        x = q_ref[0, :, p * LANES:(p + 1) * LANES]
        qo_ref[0, :, p * LANES:(p + 1) * LANES] = (_rope(_head_rms(x, gq_ref[...]), cos, sin) * scale).astype(BF16)

    low = _lane_iota(cos.shape) < HEAD_DIM

    def dup(x, o_ref):
        sw = pltpu.roll(x, HEAD_DIM, 1)
        o_ref[0, 0] = jnp.where(low, x, sw).astype(BF16)
        o_ref[0, 1] = jnp.where(low, sw, x).astype(BF16)

    dup(_rope(_head_rms(ks_ref[0], gs_ref[...]), cos, sin), kso_ref)
    dup(_rope(_head_rms(kw_ref[0], gw_ref[...]), cos, sin), kwo_ref)
    dup(vs_ref[0], vso_ref)
    dup(vw_ref[0], vwo_ref)


def _nsa_prep(proj3, cos, sin, gq, gs, gw, ts):
    b, s, _ = proj3.shape

    def col(c):
        return pl.BlockSpec((1, ts, LANES), lambda bi, i: (bi, i, c // LANES))

    tab = pl.BlockSpec((1, ts, LANES), lambda bi, i: (bi, i, 0))
    gain = pl.BlockSpec((1, LANES), lambda bi, i: (0, 0))
    kv_out = pl.BlockSpec((1, NSA_KV_HEADS, ts, LANES), lambda bi, i: (bi, 0, i, 0))
    kv_shape = jax.ShapeDtypeStruct((b, NSA_KV_HEADS, s, LANES), BF16)
    return pl.pallas_call(
        _nsa_prep_kernel,
        out_shape=(jax.ShapeDtypeStruct((b, s, NSA_WIDTH), BF16), kv_shape, kv_shape, kv_shape, kv_shape),
        grid=(b, s // ts),
        in_specs=[pl.BlockSpec((1, ts, NSA_WIDTH), lambda bi, i: (bi, i, COL_N_Q // NSA_WIDTH)),
                  col(COL_N_KS), col(COL_N_VS), col(COL_N_KW), col(COL_N_VW), tab, tab, gain, gain, gain],
        out_specs=(pl.BlockSpec((1, ts, NSA_WIDTH), lambda bi, i: (bi, i, 0)), kv_out, kv_out, kv_out, kv_out),
        compiler_params=_cparams(("parallel", "parallel")),
        name="nsa_prep",
    )(proj3, proj3, proj3, proj3, proj3, cos, sin, gq, gs, gw)


def _cmp_kernel(xk_ref, xv_ref, pos_ref, freq_ref, pe_ref, w1_ref, b1_ref, w2_ref, gk_ref, kc_ref, vc_ref):
    half = CMP_STRIDE * HEAD_DIM

    def mlp(x, kv):
        top = (x + pe_ref[kv, 0]).astype(BF16)
        bot = (x + pe_ref[kv, 1]).astype(BF16)
        a = _dot(top, w1_ref[kv, 0:half, :])
        bm = _dot(bot, w1_ref[kv, half:, :])
        n = a.shape[0]
        h = a + pltpu.roll(bm, n - 1, 0) + b1_ref[kv]
        h = h * jax.nn.sigmoid(h)
        return _dot(h.astype(BF16), w2_ref[kv])

    kc = mlp(xk_ref[0, 0], 0)
    vc = mlp(xv_ref[0, 0], 1)
    kc = kc * lax.rsqrt(jnp.mean(kc * kc, axis=-1, keepdims=True) + NORM_EPS) * gk_ref[...]
    psum = jnp.sum(pos_ref[0], axis=-1, keepdims=True)
    n = psum.shape[0]
    mean_pos = (psum + pltpu.roll(psum, n - 1, 0)) * (1.0 / CMP_LEN)
    ang = mean_pos * freq_ref[...]
    first = (_lane_iota(ang.shape) % HEAD_DIM) < (HEAD_DIM // 2)
    kc = _rope(kc, jnp.cos(ang), jnp.where(first, -jnp.sin(ang), jnp.sin(ang)))
    kc_ref[0, 0] = kc.astype(BF16)
    vc_ref[0, 0] = vc.astype(BF16)


def _compress(xk, xv, pos16, freq, pe, w1, b1, w2, gk):
    b, g, n, width = xk.shape
    x_spec = pl.BlockSpec((1, 1, n, width), lambda bi, gi: (bi, gi, 0, 0))
    out_spec = pl.BlockSpec((1, 1, n, LANES), lambda bi, gi: (bi, gi, 0, 0))
    out_shape = jax.ShapeDtypeStruct((b, g, n, LANES), BF16)

    def whole(a):
        nd = a.ndim
        return pl.BlockSpec(a.shape, lambda bi, gi: (0,) * nd)

    return pl.pallas_call(
        _cmp_kernel,
        out_shape=(out_shape, out_shape),
        grid=(b, g),
        in_specs=[x_spec, x_spec, pl.BlockSpec((1, n, CMP_STRIDE), lambda bi, gi: (bi, 0, 0)),
                  whole(freq), whole(pe), whole(w1), whole(b1), whole(w2), whole(gk)],
        out_specs=(out_spec, out_spec),
        compiler_params=_cparams(("parallel", "parallel")),
        name="nsa_compress",
    )(xk, xv, pos16, freq, pe, w1, b1, w2, gk)


def _nsa_kernel(q_ref, ks_ref, vs_ref, kw_ref, vw_ref, kc_ref, vc_ref, gate_ref, ov_ref, ex_ref, o_ref, *, tq):
    i = pl.program_id(2)
    s_len = ks_ref.shape[2]
    r4 = NSA_GROUP * tq
    low1 = _lane_iota((1, LANES)) < HEAD_DIM

    heads = []
    for p in range(NSA_GROUP // 2):
        qp = q_ref[0, :, p * LANES:(p + 1) * LANES]
        heads.append(jnp.where(low1, qp, jnp.zeros_like(qp)))
        heads.append(jnp.where(low1, jnp.zeros_like(qp), qp))
    q4 = jnp.concatenate(heads, axis=0)

    t_col = i * tq + lax.broadcasted_iota(jnp.int32, (tq, 1), 0)
    t4 = jnp.concatenate([t_col] * NSA_GROUP, axis=0)

    lane_c = _lane_iota((1, LANES))
    sc = _dot_nt(q4, kc_ref[0, 0])
    mask_c = (lane_c * CMP_STRIDE + (CMP_LEN - 1)) <= t4
    sc = jnp.where(mask_c, sc, NEG_INF)
    pc = jnp.exp(sc - jnp.max(sc, axis=-1, keepdims=True))
    pc = pc / jnp.sum(pc, axis=-1, keepdims=True)
    pc = jnp.where(t4 >= CMP_LEN - 1, pc, 0.0)
    o_cmp = _dot(pc.astype(BF16), vc_ref[0, 0])

    psum = pc[0:tq]
    for h in range(1, NSA_GROUP):
        psum = psum + pc[h * tq:(h + 1) * tq]
    imp = _split_dot(psum, ov_ref[...])
    lane_j = _lane_iota((tq, LANES))
    cur = t_col // SEL_BLOCK
    forced = (lane_j == 0) | (lane_j == cur) | (lane_j == cur - 1)
    imp = jnp.where(lane_j * SEL_BLOCK <= t_col, imp + jnp.where(forced, FORCED_BONUS, 0.0), NEG_INF)
    n_sel = s_len // SEL_BLOCK
    rank = jnp.zeros((tq, LANES), F32)
    for j in range(n_sel):
        cj = imp[:, j:j + 1]
        beats = (cj > imp) | ((cj == imp) & (lane_j > j))
        rank = rank + jnp.where(beats, 1.0, 0.0)
    sel = jnp.where((rank < min(SEL_TOP_N, n_sel)) & (lane_j < n_sel), 1.0, 0.0).astype(BF16)

    def slc_step(c, carry):
        m, l, acc = carry
        k0 = pl.multiple_of(c * SLC_CHUNK, SLC_CHUNK)
        s = _dot_nt(q4, ks_ref[0, 0, pl.ds(k0, SLC_CHUNK), :])
        picked = _dot(sel, ex_ref[c])
        kpos = k0 + _lane_iota((tq, SLC_CHUNK))
        bias = jnp.where((picked > 0.5) & (kpos <= t_col), 0.0, NEG_INF)
        s = (s.reshape(NSA_GROUP, tq, SLC_CHUNK) + bias[None]).reshape(r4, SLC_CHUNK)
        m_new = jnp.maximum(m, jnp.max(s, axis=-1, keepdims=True))
        alpha = jnp.exp(m - m_new)
        p = jnp.exp(s - m_new)
        l = alpha * l + jnp.sum(p, axis=-1, keepdims=True)
        acc = alpha * acc + _dot(p.astype(BF16), vs_ref[0, 0, pl.ds(k0, SLC_CHUNK), :])
        return m_new, l, acc

    n_chunks = ((i + 1) * tq + SLC_CHUNK - 1) // SLC_CHUNK
    init = (jnp.full((r4, 1), NEG_INF, F32), jnp.zeros((r4, 1), F32), jnp.zeros((r4, LANES), F32))
    _, l_s, acc_s = lax.fori_loop(0, n_chunks, slc_step, init)

    wlen = tq + WINDOW
    w0 = pl.multiple_of(jnp.maximum(i * tq - WINDOW, 0), KEY_BLOCK)
    sw = _dot_nt(q4, kw_ref[0, 0, pl.ds(w0, wlen), :])
    kpos_w = w0 + _lane_iota((1, wlen))
    sw = jnp.where((kpos_w <= t4) & (kpos_w > t4 - WINDOW), sw, NEG_INF)
    pw = jnp.exp(sw - jnp.max(sw, axis=-1, keepdims=True))
    l_w = jnp.sum(pw, axis=-1, keepdims=True)
    acc_w = _dot(pw.astype(BF16), vw_ref[0, 0, pl.ds(w0, wlen), :])

    gates = jax.nn.sigmoid(gate_ref[0])
    outs = []
    for h in range(NSA_GROUP):
        rows = slice(h * tq, (h + 1) * tq)
        g_c = gates[:, h:h + 1]
        g_s = gates[:, NSA_GROUP + h:NSA_GROUP + h + 1]
        g_w = gates[:, 2 * NSA_GROUP + h:2 * NSA_GROUP + h + 1]
        outs.append(g_c * o_cmp[rows] + (g_s / l_s[rows]) * acc_s[rows] + (g_w / l_w[rows]) * acc_w[rows])
    low = _lane_iota((tq, LANES)) < HEAD_DIM
    for p in range(NSA_GROUP // 2):
        o_ref[0, :, p * LANES:(p + 1) * LANES] = jnp.where(low, outs[2 * p], outs[2 * p + 1])


def _nsa_attention(q_rot, ks, vs, kw, vw, kc, vc, proj3, overlap, expand, tq):
    b, s, _ = q_rot.shape
    gw = NSA_GROUP * HEAD_DIM
    kv_spec = pl.BlockSpec((1, 1, s, LANES), lambda bi, g, i: (bi, g, 0, 0))
    c_spec = pl.BlockSpec((1, 1, kc.shape[2], LANES), lambda bi, g, i: (bi, g, 0, 0))
    return pl.pallas_call(
        functools.partial(_nsa_kernel, tq=tq),
        out_shape=jax.ShapeDtypeStruct((b, s, NSA_WIDTH), F32),
        grid=(b, NSA_KV_HEADS, s // tq),
        in_specs=[pl.BlockSpec((1, tq, gw), lambda bi, g, i: (bi, i, g)),
                  kv_spec, kv_spec, kv_spec, kv_spec, c_spec, c_spec,
                  pl.BlockSpec((1, tq, LANES), lambda bi, g, i: (bi, i, COL_N_GATE // LANES + g)),
                  pl.BlockSpec(overlap.shape, lambda bi, g, i: (0, 0)),
                  pl.BlockSpec(expand.shape, lambda bi, g, i: (0, 0, 0))],
        out_specs=pl.BlockSpec((1, tq, gw), lambda bi, g, i: (bi, i, g)),
        compiler_params=_cparams(("parallel", "parallel", "parallel")),
        name="nsa_attention",
    )(q_rot, ks, vs, kw, vw, kc, vc, proj3, overlap, expand)


def _out_kernel(x_ref, oa_ref, ob_ref, za_ref, zb_ref, ga_ref, gb_ref, wa_ref, wb_ref, wo_ref, o_ref):
    za = za_ref[...]
    zb = zb_ref[...]
    ya = oa_ref[...] * (za * jax.nn.sigmoid(za))
    yb = ob_ref[...] * (zb * jax.nn.sigmoid(zb))
    merged = (jax.nn.sigmoid(ga_ref[...]) * _dot(ya.astype(BF16), wa_ref[...])
              + jax.nn.sigmoid(gb_ref[...]) * _dot(yb.astype(BF16), wb_ref[...]))
    o_ref[...] = x_ref[...] + _dot(merged.astype(BF16), wo_ref[...])


def _merge_out(x2d, oa2d, ob2d, proj, wa, wb, wo, tm):
    m = x2d.shape[0]

    def rows(width, c=0):
        return pl.BlockSpec((tm, width), lambda i: (i, c // width))

    def whole(a):
        return pl.BlockSpec(a.shape, lambda i: (0, 0))

    return pl.pallas_call(
        _out_kernel,
        out_shape=jax.ShapeDtypeStruct((m, D_MODEL), F32),
        grid=(m // tm,),
        in_specs=[rows(D_MODEL), rows(SB_WIDTH), rows(NSA_WIDTH),
                  rows(SB_WIDTH, COL_SB_Z), rows(NSA_WIDTH, COL_N_Z),
                  rows(D_MODEL, COL_GATE_A), rows(D_MODEL, COL_GATE_B),
                  whole(wa), whole(wb), whole(wo)],
        out_specs=rows(D_MODEL),
        compiler_params=_cparams(("parallel",)),
        name="merge_out_proj",
    )(x2d, oa2d, ob2d, proj, proj, proj, proj, wa, wb, wo)


def _suffix_operator():
    r = jnp.arange(2 * KEY_BLOCK)[:, None]
    c = jnp.arange(2 * KEY_BLOCK)[None, :]
    same = (r // KEY_BLOCK) == (c // KEY_BLOCK)
    incl = same & (r >= c)
    return jnp.concatenate([incl, same], axis=1).astype(BF16)


def _overlap_operator(s_len):
    n_cmp = (s_len - CMP_LEN) // CMP_STRIDE + 1
    n_sel = s_len // SEL_BLOCK
    c0 = jnp.arange(LANES)[:, None] * CMP_STRIDE
    j0 = jnp.arange(LANES)[None, :] * SEL_BLOCK
    ov = (c0 < j0 + SEL_BLOCK) & (c0 + CMP_LEN > j0)
    ov = ov & (jnp.arange(LANES)[:, None] < n_cmp) & (jnp.arange(LANES)[None, :] < n_sel)
    return ov.astype(BF16)


def _expand_operator(s_len):
    j = jnp.arange(LANES)[None, :, None]
    s = (jnp.arange(s_len // SLC_CHUNK)[:, None, None] * SLC_CHUNK + jnp.arange(SLC_CHUNK)[None, None, :])
    return (j == s // SEL_BLOCK).astype(BF16)


def _pad_w_in(w_in):
    depth = w_in.shape[0]
    gate = w_in[:, :, N_IN_ORIG_GATE:N_IN_ORIG_GATE + N_GATE]
    gate = gate.reshape(depth, D_MODEL, NSA_BRANCHES, NSA_KV_HEADS, NSA_GROUP).transpose(0, 1, 3, 2, 4)
    gate = gate.reshape(depth, D_MODEL, NSA_KV_HEADS, NSA_BRANCHES * NSA_GROUP)
    gate = jnp.pad(gate, ((0, 0), (0, 0), (0, 0), (0, LANES - NSA_BRANCHES * NSA_GROUP)))
    gate = gate.reshape(depth, D_MODEL, NSA_KV_HEADS * LANES)
    return jnp.concatenate([w_in[:, :, :N_IN_ORIG_GATE], gate, w_in[:, :, N_IN_ORIG_GATE + N_GATE:]],
                           axis=-1).astype(BF16)


def kernel(x, positions, norm_g, w_in, q_norm_g, k_norm_g, cmp_pe, cmp_w1, cmp_b1, cmp_w2, w_up_a, w_up_b, w_out):
    b, s, d = x.shape
    depth = w_in.shape[0]
    assert d == D_MODEL and s % SLC_CHUNK == 0 and s // CMP_STRIDE == LANES and s >= WINDOW + 256
    m = b * s
    tq_sb, tq_nsa, ts_prep, tm_proj, tm_out = 256, 128, 512, 256, 512

    half = HEAD_DIM // 2
    inv_freq = ROPE_THETA ** (-jnp.arange(half, dtype=F32) / half)
    freq = jnp.tile(inv_freq, LANES // half)[None, :]
    pos_f = positions.astype(F32)
    cos, sin = _rope_tables(pos_f[:, :, None], freq, ts_prep)
    pos16 = pos_f.reshape(b, s // CMP_STRIDE, CMP_STRIDE)

    u_op = _suffix_operator()
    ov_op = _overlap_operator(s)
    ex_op = _expand_operator(s)

    w_in_p = _pad_w_in(w_in)
    two = LANES // HEAD_DIM
    gq = jnp.tile(q_norm_g, (1, two))[:, None, :]
    gk = jnp.tile(k_norm_g, (1, 1, two))
    pe = cmp_pe.reshape(depth, 2, 2, 1, CMP_STRIDE * HEAD_DIM)
    w1 = cmp_w1.astype(BF16)
    b1 = cmp_b1[:, :, None, :]
    w2 = jnp.tile(cmp_w2, (1, 1, 1, two)).astype(BF16)
    layers = (norm_g[:, None, :], w_in_p, gq, gk, pe, w1, b1, w2,
              w_up_a.astype(BF16), w_up_b.astype(BF16), w_out.astype(BF16))

    def to_blocks(a):
        a = a.reshape(b, s // CMP_STRIDE, CMP_STRIDE, NSA_KV_HEADS, HEAD_DIM)
        return a.transpose(0, 3, 1, 2, 4).reshape(b, NSA_KV_HEADS, s // CMP_STRIDE, CMP_STRIDE * HEAD_DIM)

    def layer(xc, p):
        ng, wi, gq_l, gk_l, pe_l, w1_l, b1_l, w2_l, wa, wb, wo = p
        proj = _proj(xc, ng, wi, tm_proj)
        proj3 = proj.reshape(b, s, N_PAD)
        o_a = _sb_attention(proj3, u_op, tq_sb)
        q_rot, ks, vs, kw, vw = _nsa_prep(proj3, cos, sin, gq_l, gk_l[1:2], gk_l[2:3], ts_prep)
        xk = to_blocks(proj3[:, :, COL_N_KC:COL_N_KC + NSA_KV_WIDTH])
        xv = to_blocks(proj3[:, :, COL_N_VC:COL_N_VC + NSA_KV_WIDTH])
        kc, vc = _compress(xk, xv, pos16, freq, pe_l, w1_l, b1_l, w2_l, gk_l[0:1])
        o_b = _nsa_attention(q_rot, ks, vs, kw, vw, kc, vc, proj3, ov_op, ex_op, tq_nsa)
        out = _merge_out(xc, o_a.reshape(m, SB_WIDTH), o_b.reshape(m, NSA_WIDTH), proj, wa, wb, wo, tm_out)
        return out, None

    out, _ = lax.scan(layer, x.reshape(m, d), layers)
    return out.reshape(b, s, d)
```
